```python
import math
import jax, jax.numpy as jnp
from jax import lax
import numpy as np

D_MODEL = 1024
BATCH = 8
SEQ = 2048
DEPTH = 2
DEC_BATCH = 32
DEC_SEQ = 4
PAST_LEN = 8192
PAGE_SIZE = 128

N_EVEN = (DEPTH + 1) // 2
N_ODD = DEPTH // 2
CONV_CH_A = D_MODEL // 2
CONV_CH_B = D_MODEL // 2
IN_AB = 2 * CONV_CH_A + 3 * CONV_CH_B
KA = 31
KB = 3
KF = 3
HEAD_DIM = 64
N_HEADS = D_MODEL // HEAD_DIM
D_FF = ((8 * D_MODEL // 3 + 255) // 256) * 256
BRANCHES = ((128, 1), (512, 4), (2048, 16))
MAX_WINDOW = 2048
Q_BLOCK = 128
RMS_EPS = 1e-6
LN_EPS = 1e-5

kernel_name = "hybrid_conformerconv_shortconv_dilatedswa_convffn_step"


def rms_norm(x, g):
    xf = x.astype(jnp.float32)
    y = xf * lax.rsqrt(jnp.mean(xf * xf, axis=-1, keepdims=True) + RMS_EPS)
    return (y * g.astype(jnp.float32)).astype(x.dtype)


def causal_dwconv(x, state, w):
    k = w.shape[0]
    xp = jnp.concatenate([state.astype(x.dtype), x], axis=1)
    y = lax.conv_general_dilated(xp, w.astype(x.dtype)[:, None, :], (1,), 'VALID',
                                 dimension_numbers=('NWC', 'WIO', 'NWC'),
                                 feature_group_count=x.shape[-1])
    return y, xp[:, xp.shape[1] - (k - 1):]


def conv_mixers(h, st_a, st_b, w_in, wa, ba, lg, lb, wb, w_out):
    z = h @ w_in
    ca, cb = CONV_CH_A, CONV_CH_B
    glu = z[..., :ca] * jax.nn.sigmoid(z[..., ca:2 * ca])
    a, st_a_new = causal_dwconv(glu, st_a, wa)
    af = (a + ba.astype(a.dtype)).astype(jnp.float32)
    mu = jnp.mean(af, axis=-1, keepdims=True)
    var = jnp.mean(jnp.square(af - mu), axis=-1, keepdims=True)
    an = (af - mu) * lax.rsqrt(var + LN_EPS) * lg.astype(jnp.float32) + lb.astype(jnp.float32)
    a_out = (an * jax.nn.sigmoid(an)).astype(h.dtype)
    o = 2 * ca
    gate_b = z[..., o:o + cb]
    gate_c = z[..., o + cb:o + 2 * cb]
    x_t = z[..., o + 2 * cb:o + 3 * cb]
    cbo, st_b_new = causal_dwconv(gate_c * x_t, st_b, wb)
    b_out = gate_b * cbo
    y = jnp.concatenate([a_out, b_out], axis=-1) @ w_out
    return y, st_a_new, st_b_new


def conv_ffn(h, st, w_in, wc, w_down):
    z = h @ w_in
    g, st_new = causal_dwconv(z[..., :D_FF], st, wc)
    return (jax.nn.silu(g) * z[..., D_FF:]) @ w_down, st_new


def alibi_slopes():
    return jnp.asarray(2.0 ** (-8.0 * np.arange(1, N_HEADS + 1) / N_HEADS), dtype=jnp.float32)


def qkv_split(h, w_qkv):
    b, t, _ = h.shape
    z = (h @ w_qkv).reshape(b, t, 3, N_HEADS, HEAD_DIM)
    return z[:, :, 0], z[:, :, 1], z[:, :, 2]


def dilated_branch_prompt(q, k, v, steps, dil, slopes):
    bsz, s_len, nh, hd = q.shape
    sd = s_len // dil
    qb = math.gcd(sd, Q_BLOCK)
    nb = sd // qb
    qf = q.astype(jnp.float32).reshape(bsz, nb, qb, dil, nh, hd)
    pad = ((0, 0), (steps, 0), (0, 0), (0, 0), (0, 0))
    kp = jnp.pad(k.astype(jnp.float32).reshape(bsz, sd, dil, nh, hd), pad)
    vp = jnp.pad(v.astype(jnp.float32).reshape(bsz, sd, dil, nh, hd), pad)
    start = jnp.arange(nb) * qb
    idx = start[:, None] + jnp.arange(qb + steps)[None, :]
    kb = kp[:, idx]
    vb = vp[:, idx]
    s = jnp.einsum('bnirhc,bnjrhc->bnrhij', qf, kb) * (1.0 / math.sqrt(hd))
    ii = jnp.arange(qb)[:, None]
    jj = jnp.arange(qb + steps)[None, :]
    rel = ii + steps - jj
    valid = (rel >= 0) & (rel <= steps) & ((start[:, None, None] + jj - steps) >= 0)[:, :, :]
    bias = -slopes[:, None, None] * (dil * rel).astype(jnp.float32)[None]
    s = jnp.where(valid[None, :, None, None], s + bias, -jnp.inf)
    m = jnp.max(s, axis=-1, keepdims=True)
    p = jnp.exp(s - m)
    l = jnp.sum(p, axis=-1)
    o = jnp.einsum('bnrhij,bnjrhc->bnirhc', p, vb)
    lt = jnp.transpose(l, (0, 1, 4, 2, 3))
    o = (o / lt[..., None]).reshape(bsz, s_len, nh, hd)
    lse = (jnp.transpose(m[..., 0], (0, 1, 4, 2, 3)) + jnp.log(lt)).reshape(bsz, s_len, nh)
    return o, lse


def dilated_branch_sample(q, kc, vc, buf_len, steps, dil, slopes):
    t = q.shape[1]
    hd = q.shape[-1]
    msteps = jnp.arange(steps + 1)
    idx = buf_len + jnp.arange(t)[:, None] - dil * msteps[None, :]
    valid = idx >= 0
    idxc = jnp.maximum(idx, 0)
    kg = kc.astype(jnp.float32)[:, idxc]
    vg = vc.astype(jnp.float32)[:, idxc]
    s = jnp.einsum('bthc,btmhc->bhtm', q.astype(jnp.float32), kg) * (1.0 / math.sqrt(hd))
    s = s - slopes[:, None, None] * (dil * msteps).astype(jnp.float32)[None, None, :]
    s = jnp.where(valid[None, None], s, -jnp.inf)
    m = jnp.max(s, axis=-1, keepdims=True)
    p = jnp.exp(s - m)
    l = jnp.sum(p, axis=-1)
    lt = jnp.transpose(l, (0, 2, 1))
    o = jnp.einsum('bhtm,btmhc->bthc', p, vg) / lt[..., None]
    lse = jnp.transpose(m[..., 0], (0, 2, 1)) + jnp.log(lt)
    return o, lse


def merge_branches(outs, lses):
    w = jax.nn.softmax(jnp.stack(lses, axis=0), axis=0)
    return jnp.sum(w[..., None] * jnp.stack(outs, axis=0), axis=0)


def dilated_attention_prompt(q, k, v, slopes):
    outs, lses = [], []
    for window, dil in BRANCHES:
        o, l = dilated_branch_prompt(q, k, v, window // dil, dil, slopes)
        outs.append(o)
        lses.append(l)
    return merge_branches(outs, lses)


def dilated_attention_sample(q, kc, vc, buf_len, slopes):
    outs, lses = [], []
    for window, dil in BRANCHES:
        o, l = dilated_branch_sample(q, kc, vc, buf_len, window // dil, dil, slopes)
        outs.append(o)
        lses.append(l)
    return merge_branches(outs, lses)


def setup_inputs(seed: int = 0) -> dict:
    key = jax.random.key(seed)
    ks = jax.random.split(key, 24)
    win_len = min(MAX_WINDOW, PAST_LEN)
    f32 = jnp.float32
    nrm = lambda k, shape, scale: jax.random.normal(k, shape, f32) * scale
    return {
        "x_prompt": nrm(ks[0], (BATCH, SEQ, D_MODEL), 1.0),
        "x_sample": nrm(ks[1], (DEC_BATCH, DEC_SEQ, D_MODEL), 1.0),
        "state_conv_a": nrm(ks[2], (N_EVEN, DEC_BATCH, KA - 1, CONV_CH_A), 0.5),
        "state_conv_b": nrm(ks[3], (N_EVEN, DEC_BATCH, KB - 1, CONV_CH_B), 1.0),
        "cache_k": nrm(ks[4], (N_ODD, DEC_BATCH, win_len, N_HEADS, HEAD_DIM), 1.0),
        "cache_v": nrm(ks[5], (N_ODD, DEC_BATCH, win_len, N_HEADS, HEAD_DIM), 1.0),
        "state_ffn": nrm(ks[6], (DEPTH, DEC_BATCH, KF - 1, D_FF), 1.0),
        "g_mix": 1.0 + nrm(ks[7], (DEPTH, D_MODEL), 0.02),
        "g_ffn": 1.0 + nrm(ks[8], (DEPTH, D_MODEL), 0.02),
        "g_final": 1.0 + nrm(ks[9], (D_MODEL,), 0.02),
        "w_in_ab": nrm(ks[10], (N_EVEN, D_MODEL, IN_AB), D_MODEL ** -0.5),
        "conv_a_w": nrm(ks[11], (N_EVEN, KA, CONV_CH_A), KA ** -0.5),
        "conv_a_b": nrm(ks[12], (N_EVEN, CONV_CH_A), 0.02),
        "ln_a_g": 1.0 + nrm(ks[13], (N_EVEN, CONV_CH_A), 0.02),
        "ln_a_b": nrm(ks[14], (N_EVEN, CONV_CH_A), 0.02),
        "conv_b_w": nrm(ks[15], (N_EVEN, KB, CONV_CH_B), KB ** -0.5),
        "w_out_ab": nrm(ks[16], (N_EVEN, CONV_CH_A + CONV_CH_B, D_MODEL), (CONV_CH_A + CONV_CH_B) ** -0.5),
        "w_qkv": nrm(ks[17], (N_ODD, D_MODEL, 3 * N_HEADS * HEAD_DIM), D_MODEL ** -0.5),
        "w_o": nrm(ks[18], (N_ODD, N_HEADS * HEAD_DIM, D_MODEL), (N_HEADS * HEAD_DIM) ** -0.5),
        "w_ffn_in": nrm(ks[19], (DEPTH, D_MODEL, 2 * D_FF), D_MODEL ** -0.5),
        "conv_f_w": nrm(ks[20], (DEPTH, KF, D_FF), KF ** -0.5),
        "w_down": nrm(ks[21], (DEPTH, D_FF, D_MODEL), D_FF ** -0.5),
    }


def reference(x_prompt, x_sample, state_conv_a, state_conv_b, cache_k, cache_v, state_ffn,
              g_mix, g_ffn, g_final, w_in_ab, conv_a_w, conv_a_b, ln_a_g, ln_a_b, conv_b_w, w_out_ab,
              w_qkv, w_o, w_ffn_in, conv_f_w, w_down):
    slopes = alibi_slopes()
    bp, bs = x_prompt.shape[0], x_sample.shape[0]
    buf_len = cache_k.shape[2]
    hp, hs = x_prompt, x_sample
    pa, sa, pb, sb, pk, sk, pv, sv, pf, sf = [], [], [], [], [], [], [], [], [], []
    for layer in range(DEPTH):
        i = layer // 2
        xn_p = rms_norm(hp, g_mix[layer])
        xn_s = rms_norm(hs, g_mix[layer])
        if layer % 2 == 0:
            za = jnp.zeros((bp, KA - 1, CONV_CH_A), hp.dtype)
            zb = jnp.zeros((bp, KB - 1, CONV_CH_B), hp.dtype)
            yp, na, nb_ = conv_mixers(xn_p, za, zb, w_in_ab[i], conv_a_w[i], conv_a_b[i],
                                      ln_a_g[i], ln_a_b[i], conv_b_w[i], w_out_ab[i])
            ys, ma, mb = conv_mixers(xn_s, state_conv_a[i], state_conv_b[i], w_in_ab[i], conv_a_w[i],
                                     conv_a_b[i], ln_a_g[i], ln_a_b[i], conv_b_w[i], w_out_ab[i])
            pa.append(na)
            pb.append(nb_)
            sa.append(ma)
            sb.append(mb)
        else:
            q, k, v = qkv_split(xn_p, w_qkv[i])
            att = dilated_attention_prompt(q, k, v, slopes)
            yp = att.reshape(bp, hp.shape[1], N_HEADS * HEAD_DIM).astype(hp.dtype) @ w_o[i]
            keep = min(MAX_WINDOW, k.shape[1])
            pk.append(k[:, k.shape[1] - keep:])
            pv.append(v[:, v.shape[1] - keep:])
            qs, ks_, vs_ = qkv_split(xn_s, w_qkv[i])
            kc = jnp.concatenate([cache_k[i].astype(ks_.dtype), ks_], axis=1)
            vc = jnp.concatenate([cache_v[i].astype(vs_.dtype), vs_], axis=1)
            att_s = dilated_attention_sample(qs, kc, vc, buf_len, slopes)
            ys = att_s.reshape(bs, hs.shape[1], N_HEADS * HEAD_DIM).astype(hs.dtype) @ w_o[i]
            sk.append(kc[:, kc.shape[1] - buf_len:])
            sv.append(vc[:, vc.shape[1] - buf_len:])
        hp = hp + yp
        hs = hs + ys
        zf = jnp.zeros((bp, KF - 1, D_FF), hp.dtype)
        fp, nfp = conv_ffn(rms_norm(hp, g_ffn[layer]), zf, w_ffn_in[layer], conv_f_w[layer], w_down[layer])
        fs, nfs = conv_ffn(rms_norm(hs, g_ffn[layer]), state_ffn[layer], w_ffn_in[layer], conv_f_w[layer], w_down[layer])
        hp = hp + fp
        hs = hs + fs
        pf.append(nfp)
        sf.append(nfs)
    y_prompt = rms_norm(hp, g_final)
    y_sample = rms_norm(hs, g_final)
    p_conv_a = jnp.stack(pa)
    s_conv_a = jnp.stack(sa)
    p_conv_b = jnp.stack(pb)
    s_conv_b = jnp.stack(sb)
    p_k = jnp.stack(pk)
    s_k = jnp.stack(sk)
    p_v = jnp.stack(pv)
    s_v = jnp.stack(sv)
    p_ffn = jnp.stack(pf)
    s_ffn = jnp.stack(sf)
    return (y_prompt, y_sample, p_conv_a, s_conv_a, p_conv_b, s_conv_b, p_k, s_k, p_v, s_v, p_ffn, s_ffn)
```

```python
import functools

import numpy as np
import jax
import jax.numpy as jnp
from jax import lax
from jax.experimental import pallas as pl
from jax.experimental.pallas import tpu as pltpu

D_MODEL = 1024
CONV_CH = 512
KA, KB, KF = 31, 3, 3
HEAD_DIM = 64
N_HEADS = D_MODEL // HEAD_DIM
N_PAIRS = N_HEADS // 2
D_FF = 2816
FF_CHUNK = 256
N_FF_CHUNKS = D_FF // FF_CHUNK
BRANCHES = ((128, 1), (512, 4), (2048, 16))
Q_BLOCK = 128
RMS_EPS = 1e-6
LN_EPS = 1e-5
NEG_BIG = -1e30

V7X_SUBLANES = 8
V7X_LANES = 128
VMEM_LIMIT_BYTES = 56 * 1024 * 1024

BF16 = jnp.bfloat16
F32 = jnp.float32


def _round_up(x, m):
    return (x + m - 1) // m * m


def _dot(a, b):
    return jnp.dot(a, b, preferred_element_type=F32)


def _dot_nt(a, b):
    return lax.dot_general(a, b, (((1,), (1,)), ((), ())), preferred_element_type=F32)


def _rms(x, g):
    return x * lax.rsqrt(jnp.mean(x * x, axis=-1, keepdims=True) + RMS_EPS) * g


def _sigmoid(x):
    return 1.0 / (1.0 + jnp.exp(-x))


def _const_spec(shape):
    nd = len(shape)
    return pl.BlockSpec(shape, lambda *_: (0,) * nd, pipeline_mode=pl.Buffered(1))


def _params(n_grid):
    return pltpu.CompilerParams(
        dimension_semantics=("arbitrary",) * n_grid, vmem_limit_bytes=VMEM_LIMIT_BYTES)


def _mixer_kernel(h_ref, sta_ref, stb_ref, g_ref, win_ref, wa_ref, ba_ref, lg_ref, lb_ref, wb_ref,
                  act_ref, nsa_ref, nsb_ref, hist_a, hist_b, gate_b, *, tt, stride, nt, rc):
    pa = hist_a.shape[0] - tt
    pb = hist_b.shape[0] - tt
    base_a = pa - (KA - 1) * stride
    base_b = pb - (KB - 1) * stride
    t = pl.program_id(1)

    @pl.when(t == 0)
    def _():
        hist_a[base_a:pa, :] = sta_ref[...]
        hist_b[base_b:pb, :] = stb_ref[...]

    xb = _rms(h_ref[...], g_ref[...]).astype(BF16)
    c = CONV_CH
    z_glu = _dot(xb, win_ref[:, 0:c])
    z_gate = _dot(xb, win_ref[:, c:2 * c])
    hist_a[pa:pa + tt, :] = z_glu * _sigmoid(z_gate)
    gate_b[...] = _dot(xb, win_ref[:, 2 * c:3 * c])
    hist_b[pb:pb + tt, :] = _dot(xb, win_ref[:, 3 * c:4 * c]) * _dot(xb, win_ref[:, 4 * c:5 * c])

    def chunk(i, carry):
        r0 = pl.multiple_of(i * rc, rc)
        win_a = hist_a.at[pl.ds(r0, rc + pa)]
        acc = wa_ref[0:1, :] * win_a[base_a:base_a + rc, :]
        for k in range(1, KA):
            off = base_a + k * stride
            acc = acc + wa_ref[k:k + 1, :] * win_a[off:off + rc, :]
        af = acc + ba_ref[...]
        mu = jnp.mean(af, axis=-1, keepdims=True)
        dev = af - mu
        var = jnp.mean(dev * dev, axis=-1, keepdims=True)
        an = dev * lax.rsqrt(var + LN_EPS) * lg_ref[...] + lb_ref[...]
        act_ref[pl.ds(r0, rc), 0:c] = (an * _sigmoid(an)).astype(BF16)

        win_b = hist_b.at[pl.ds(r0, rc + pb)]
        accb = wb_ref[0:1, :] * win_b[base_b:base_b + rc, :]
        for k in range(1, KB):
            off = base_b + k * stride
            accb = accb + wb_ref[k:k + 1, :] * win_b[off:off + rc, :]
        act_ref[pl.ds(r0, rc), c:2 * c] = (gate_b[pl.ds(r0, rc), :] * accb).astype(BF16)
        return carry

    lax.fori_loop(0, tt // rc, chunk, 0)

    @pl.when(t == nt - 1)
    def _():
        nsa_ref[...] = hist_a[pa + tt - (KA - 1) * stride:pa + tt, :]
        nsb_ref[...] = hist_b[pb + tt - (KB - 1) * stride:pb + tt, :]

    if nt > 1:
        hist_a[0:pa, :] = hist_a[tt:tt + pa, :]
        hist_b[0:pb, :] = hist_b[tt:tt + pb, :]


def _mixer_call(h, sta, stb, g, win, wa, ba, lg, lb, wb, *, tt, stride):
    nb, rows, d = h.shape
    nt = rows // tt
    assert rows % tt == 0
    rc = 32
    pa = _round_up((KA - 1) * stride, V7X_SUBLANES)
    pb = _round_up((KB - 1) * stride, V7X_SUBLANES)
    assert nt == 1 or tt >= pa
    c = CONV_CH
    tile = lambda w: pl.BlockSpec((None, tt, w), lambda b, t: (b, t, 0))
    per_b = lambda r, w: pl.BlockSpec((None, r, w), lambda b, t: (b, 0, 0))
    kern = functools.partial(_mixer_kernel, tt=tt, stride=stride, nt=nt, rc=rc)
    return pl.pallas_call(
        kern,
        grid=(nb, nt),
        in_specs=[tile(d), per_b((KA - 1) * stride, c), per_b((KB - 1) * stride, c),
                  _const_spec(g.shape), _const_spec(win.shape), _const_spec(wa.shape),
                  _const_spec(ba.shape), _const_spec(lg.shape), _const_spec(lb.shape),
                  _const_spec(wb.shape)],
        out_specs=[tile(2 * c), per_b((KA - 1) * stride, c), per_b((KB - 1) * stride, c)],
        out_shape=[jax.ShapeDtypeStruct((nb, rows, 2 * c), BF16),
                   jax.ShapeDtypeStruct((nb, (KA - 1) * stride, c), F32),
                   jax.ShapeDtypeStruct((nb, (KB - 1) * stride, c), F32)],
        scratch_shapes=[pltpu.VMEM((pa + tt, c), F32), pltpu.VMEM((pb + tt, c), F32),
                        pltpu.VMEM((tt, c), F32)],
        compiler_params=_params(2),
        name="mixer",
    )(h, sta, stb, g, win, wa, ba, lg, lb, wb)


def _ffn_kernel(h_ref, act_ref, wpre_ref, g_ref, wg_ref, wu_ref, wc_ref, wd_ref, st_ref, gfin_ref,
                o_ref, nst_ref, carry, hist, xb_s, acc, *, tt, stride, nt, final_norm):
    pf = hist.shape[0] - tt
    base = pf - (KF - 1) * stride
    t = pl.program_id(1)

    @pl.when(t == 0)
    def _():
        carry[:, base:pf, :] = st_ref[...]

    h1 = h_ref[...] + _dot(act_ref[...], wpre_ref[...])
    xb_s[...] = _rms(h1, g_ref[...]).astype(BF16)
    acc[...] = h1

    def chunk(c, _):
        xb = xb_s[...]
        hist[0:pf, :] = carry[c]
        hist[pf:pf + tt, :] = _dot(xb, wg_ref[c])
        carry[c] = hist[tt:tt + pf, :]
        wc = wc_ref[c]
        gc = wc[0:1, :] * hist[base:base + tt, :]
        for k in range(1, KF):
            off = base + k * stride
            gc = gc + wc[k:k + 1, :] * hist[off:off + tt, :]
        a = gc * _sigmoid(gc) * _dot(xb, wu_ref[c])
        acc[...] += _dot(a.astype(BF16), wd_ref[c])
        return 0

    lax.fori_loop(0, N_FF_CHUNKS, chunk, 0)

    out = acc[...]
    if final_norm:
        out = _rms(out, gfin_ref[...])
    o_ref[...] = out

    @pl.when(t == nt - 1)
    def _():
        nst_ref[...] = carry[:, base:pf, :]


def _ffn_call(h, act, wpre, g, wg, wu, wc, wd, st, gfin, *, tt, stride, final_norm):
    nb, rows, d = h.shape
    nt = rows // tt
    assert rows % tt == 0
    pf = _round_up((KF - 1) * stride, V7X_SUBLANES)
    assert tt >= pf
    srows = (KF - 1) * stride
    tile = pl.BlockSpec((None, tt, d), lambda b, t: (b, t, 0))
    st_spec = pl.BlockSpec((None, N_FF_CHUNKS, srows, FF_CHUNK), lambda b, t: (b, 0, 0, 0))
    kern = functools.partial(_ffn_kernel, tt=tt, stride=stride, nt=nt, final_norm=final_norm)
    return pl.pallas_call(
        kern,
        grid=(nb, nt),
        in_specs=[tile, tile, _const_spec(wpre.shape), _const_spec(g.shape), _const_spec(wg.shape),
                  _const_spec(wu.shape), _const_spec(wc.shape), _const_spec(wd.shape), st_spec,
                  _const_spec(gfin.shape)],
        out_specs=[tile, st_spec],
        out_shape=[jax.ShapeDtypeStruct((nb, rows, d), F32),
                   jax.ShapeDtypeStruct((nb, N_FF_CHUNKS, srows, FF_CHUNK), F32)],
        scratch_shapes=[pltpu.VMEM((N_FF_CHUNKS, pf, FF_CHUNK), F32),
                        pltpu.VMEM((pf + tt, FF_CHUNK), F32),
                        pltpu.VMEM((tt, d), BF16),
                        pltpu.VMEM((tt, d), F32)],
        compiler_params=_params(2),
        name="ffn",
    )(h, act, wpre, g, wg, wu, wc, wd, st, gfin)


def _qkv_kernel(h_ref, g_ref, w_ref, q_ref, k_ref, v_ref):
    xb = _rms(h_ref[...], g_ref[...]).astype(BF16)
    d = D_MODEL
    q_ref[...] = _dot(xb, w_ref[:, 0:d])
    k_ref[...] = _dot(xb, w_ref[:, d:2 * d])
    v_ref[...] = _dot(xb, w_ref[:, 2 * d:3 * d])


def _qkv_call(h, g, w, *, tt):
    nb, rows, d = h.shape
    assert rows % tt == 0
    tile = pl.BlockSpec((None, tt, d), lambda b, t: (b, t, 0))
    out = jax.ShapeDtypeStruct((nb, rows, d), F32)
    return pl.pallas_call(
        _qkv_kernel,
        grid=(nb, rows // tt),
        in_specs=[tile, _const_spec(g.shape), _const_spec(w.shape)],
        out_specs=[tile, tile, tile],
        out_shape=[out, out, out],
        compiler_params=_params(2),
        name="qkv",
    )(h, g, w)


def _attn_prompt_kernel(slopes_ref, q_ref, k_ref, v_ref, o_ref, acc_s, m_s, l_s, *, seq):
    pair = pl.program_id(1)
    qb = Q_BLOCK
    row2 = lax.broadcasted_iota(jnp.int32, (2 * qb, 1), 0)
    slope = jnp.where(row2 < qb, slopes_ref[2 * pair], slopes_ref[2 * pair + 1])
    lane = lax.broadcasted_iota(jnp.int32, (1, V7X_LANES), 1)
    low = lane < HEAD_DIM
    qi_idx = lax.broadcasted_iota(jnp.int32, (2 * qb, 2 * qb), 0) % qb
    kj_idx = lax.broadcasted_iota(jnp.int32, (2 * qb, 2 * qb), 1)
    rel_two = qi_idx + qb - kj_idx
    rel_one = (qi_idx - kj_idx)[:, :qb]

    def block(start, dil, first, fresh):
        rows = pl.ds(start, qb, stride=dil) if dil > 1 else pl.ds(start, qb)
        q = q_ref[rows, :] * (1.0 / np.sqrt(HEAD_DIM))
        q2 = jnp.concatenate([jnp.where(low, q, 0.0), jnp.where(low, 0.0, q)], axis=0).astype(BF16)
        if first:
            krows, rel = rows, rel_one
            valid = rel >= 0
        else:
            kstart = start - dil * qb
            krows = pl.ds(kstart, 2 * qb, stride=dil) if dil > 1 else pl.ds(kstart, 2 * qb)
            rel = rel_two
            valid = (rel >= 0) & (rel <= qb)
        kb = k_ref[krows, :].astype(BF16)
        vb = v_ref[krows, :].astype(BF16)
        s = _dot_nt(q2, kb)
        s = jnp.where(valid, s - slope * (dil * rel).astype(F32), NEG_BIG)
        m_blk = jnp.max(s, axis=-1, keepdims=True)
        m_blk = jnp.where(low, m_blk[:qb], m_blk[qb:])
        if fresh:
            m_new = m_blk
        else:
            m_old = m_s[rows, :]
            m_new = jnp.maximum(m_old, m_blk)
        m_row = jnp.concatenate([m_new[:, 0:1], m_new[:, HEAD_DIM:HEAD_DIM + 1]], axis=0)
        p = jnp.exp(s - m_row)
        l_blk = jnp.sum(p, axis=-1, keepdims=True)
        l_blk = jnp.where(low, l_blk[:qb], l_blk[qb:])
        pv = _dot(p.astype(BF16), vb)
        pv = jnp.where(low, pv[:qb], pv[qb:])
        if fresh:
            acc_s[rows, :] = pv
            l_s[rows, :] = l_blk
        else:
            alpha = jnp.exp(m_old - m_new)
            acc_s[rows, :] = alpha * acc_s[rows, :] + pv
            l_s[rows, :] = alpha * l_s[rows, :] + l_blk
        m_s[rows, :] = m_new

    for g, (window, dil) in enumerate(BRANCHES):
        assert window // dil == qb
        n_q = seq // dil // qb
        for r in range(dil):
            block(r, dil, True, g == 0)
            if n_q > 1:
                def body(qi, _, r=r, dil=dil, g=g):
                    start = r + dil * qb * qi
                    if dil == 1:
                        start = pl.multiple_of(start, qb)
                    block(start, dil, False, g == 0)
                    return 0
                lax.fori_loop(1, n_q, body, 0)

    def finish(i, _):
        rows = pl.ds(pl.multiple_of(i * 2 * qb, 2 * qb), 2 * qb)
        o_ref[rows, :] = (acc_s[rows, :] / l_s[rows, :]).astype(o_ref.dtype)
        return 0

    lax.fori_loop(0, seq // (2 * qb), finish, 0)


def _attn_prompt_call(slopes, q, k, v):
    nb, seq, d = q.shape
    slab = pl.BlockSpec((None, seq, V7X_LANES), lambda b, p: (b, 0, p))
    scratch = pltpu.VMEM((seq, V7X_LANES), F32)
    return pl.pallas_call(
        functools.partial(_attn_prompt_kernel, seq=seq),
        grid=(nb, N_PAIRS),
        in_specs=[pl.BlockSpec(memory_space=pltpu.SMEM), slab, slab, slab],
        out_specs=slab,
        out_shape=jax.ShapeDtypeStruct((nb, seq, d), BF16),
        scratch_shapes=[scratch, scratch, scratch],
        compiler_params=_params(2),
        name="attn_prompt",
    )(slopes, q, k, v)


def _attn_sample_kernel(slopes_ref, q_ref, kn_ref, vn_ref, ck_ref, cv_ref,
                        o_ref, ok_ref, ov_ref, *, buf_len, t_new, lane_blocks):
    lb = pl.program_id(1)
    rows = V7X_SUBLANES
    lane = lax.broadcasted_iota(jnp.int32, (1, V7X_LANES), 1)
    low = lane < HEAD_DIM
    row2 = lax.broadcasted_iota(jnp.int32, (2 * rows, 1), 0)
    tq = row2 % rows
    idx_c = lax.broadcasted_iota(jnp.int32, (2 * rows, buf_len), 1)
    rel_c = buf_len + tq - idx_c
    col_n = lax.broadcasted_iota(jnp.int32, (2 * rows, rows), 1)
    rel_n = tq - col_n

    def count(rel, extra=None):
        cnt = jnp.zeros(rel.shape, F32)
        for window, dil in BRANCHES:
            ok = (rel >= 0) & (rel <= window) & ((rel & (dil - 1)) == 0)
            if extra is not None:
                ok = ok & extra
            cnt = cnt + jnp.where(ok, 1.0, 0.0)
        return cnt

    cnt_c = count(rel_c)
    cnt_n = count(rel_n, col_n < t_new)

    for p in range(lane_blocks):
        ls = slice(p * V7X_LANES, (p + 1) * V7X_LANES)
        head0 = 2 * (lb * lane_blocks + p)
        slope = jnp.where(row2 < rows, slopes_ref[head0], slopes_ref[head0 + 1])
        q = q_ref[:, ls] * (1.0 / np.sqrt(HEAD_DIM))
        q2 = jnp.concatenate([jnp.where(low, q, 0.0), jnp.where(low, 0.0, q)], axis=0).astype(BF16)
        s_c = _dot_nt(q2, ck_ref[:, ls].astype(BF16)) - slope * rel_c.astype(F32)
        s_n = _dot_nt(q2, kn_ref[:, ls].astype(BF16)) - slope * rel_n.astype(F32)
        s_c = jnp.where(cnt_c > 0, s_c, NEG_BIG)
        s_n = jnp.where(cnt_n > 0, s_n, NEG_BIG)
        m = jnp.maximum(jnp.max(s_c, axis=-1, keepdims=True), jnp.max(s_n, axis=-1, keepdims=True))
        p_c = cnt_c * jnp.exp(s_c - m)
        p_n = cnt_n * jnp.exp(s_n - m)
        l = jnp.sum(p_c, axis=-1, keepdims=True) + jnp.sum(p_n, axis=-1, keepdims=True)
        pv = _dot(p_c.astype(BF16), cv_ref[:, ls].astype(BF16)) + _dot(p_n.astype(BF16), vn_ref[:, ls].astype(BF16))
        pv = pv / l
        o_ref[:, ls] = jnp.where(low, pv[:rows], pv[rows:])

    keep = buf_len - t_new
    ok_ref[0:keep, :] = ck_ref[t_new:buf_len, :]
    ok_ref[keep:buf_len, :] = kn_ref[0:t_new, :]
    ov_ref[0:keep, :] = cv_ref[t_new:buf_len, :]
    ov_ref[keep:buf_len, :] = vn_ref[0:t_new, :]


def _attn_sample_call(slopes, q, kn, vn, ck, cv, *, t_new):
    nb, rows, d = q.shape
    buf_len = ck.shape[1]
    assert rows == V7X_SUBLANES and t_new <= rows
    assert buf_len >= BRANCHES[-1][0], "every branch position must fall inside the window buffer"
    lane_blocks = 2
    width = lane_blocks * V7X_LANES
    small = pl.BlockSpec((None, rows, width), lambda b, p: (b, 0, p))
    big = pl.BlockSpec((None, buf_len, width), lambda b, p: (b, 0, p))
    return pl.pallas_call(
        functools.partial(_attn_sample_kernel, buf_len=buf_len, t_new=t_new, lane_blocks=lane_blocks),
        grid=(nb, d // width),
        in_specs=[pl.BlockSpec(memory_space=pltpu.SMEM), small, small, small, big, big],
        out_specs=[small, big, big],
        out_shape=[jax.ShapeDtypeStruct((nb, rows, d), F32),
                   jax.ShapeDtypeStruct((nb, buf_len, d), F32),
                   jax.ShapeDtypeStruct((nb, buf_len, d), F32)],
        compiler_params=_params(2),
        name="attn_sample",
    )(slopes, q, kn, vn, ck, cv)


def _pad_rows(w, rows):
    return jnp.pad(w, ((0, rows - w.shape[0]), (0, 0)))


def _ffn_weights(w_in, conv_w, w_down):
    d = w_in.shape[0]
    wg = w_in[:, :D_FF].astype(BF16).reshape(d, N_FF_CHUNKS, FF_CHUNK).transpose(1, 0, 2)
    wu = w_in[:, D_FF:].astype(BF16).reshape(d, N_FF_CHUNKS, FF_CHUNK).transpose(1, 0, 2)
    wc = _pad_rows(conv_w, V7X_SUBLANES).reshape(V7X_SUBLANES, N_FF_CHUNKS, FF_CHUNK).transpose(1, 0, 2)
    wd = w_down.astype(BF16).reshape(N_FF_CHUNKS, FF_CHUNK, w_down.shape[1])
    return wg, wu, wc, wd


def _to_time_major(x):
    b, t, c = x.shape
    return x.transpose(1, 0, 2).reshape(1, t * b, c)


def _from_time_major(x, b):
    _, rows, c = x.shape
    return x.reshape(rows // b, b, c).transpose(1, 0, 2)


def _ffn_state_in(st):
    nb, rows, _ = st.shape
    return st.reshape(nb, rows, N_FF_CHUNKS, FF_CHUNK).transpose(0, 2, 1, 3)


def _ffn_state_out(st):
    nb, _, rows, _ = st.shape
    return st.transpose(0, 2, 1, 3).reshape(nb, rows, D_FF)


def kernel(x_prompt, x_sample, state_conv_a, state_conv_b, cache_k, cache_v, state_ffn, g_mix, g_ffn, g_final, w_in_ab, conv_a_w, conv_a_b, ln_a_g, ln_a_b, conv_b_w, w_out_ab, w_qkv, w_o, w_ffn_in, conv_f_w, w_down):
    bp, seq, d = x_prompt.shape
    bs, t_new, _ = x_sample.shape
    buf_len = cache_k.shape[2]
    tt_p = 512
    rows_s = bs * t_new
    slopes = jnp.asarray(2.0 ** (-8.0 * np.arange(1, N_HEADS + 1) / N_HEADS), dtype=F32)
    row = lambda v: v.reshape(1, -1)

    hp = x_prompt
    hs = _to_time_major(x_sample)
    gfin = row(g_final)

    win = w_in_ab[0].astype(BF16)
    wa = _pad_rows(conv_a_w[0], _round_up(KA, V7X_SUBLANES))
    wb = _pad_rows(conv_b_w[0], V7X_SUBLANES)
    mix_args = (row(g_mix[0]), win, wa, row(conv_a_b[0]), row(ln_a_g[0]), row(ln_a_b[0]), wb)
    zeros_a = jnp.zeros((bp, KA - 1, CONV_CH), F32)
    zeros_b = jnp.zeros((bp, KB - 1, CONV_CH), F32)
    act_p, pa, pb = _mixer_call(hp, zeros_a, zeros_b, *mix_args, tt=tt_p, stride=1)
    act_s, sa, sb = _mixer_call(hs, _to_time_major(state_conv_a[0]), _to_time_major(state_conv_b[0]),
                                *mix_args, tt=rows_s, stride=bs)
    p_conv_a = pa[None]
    p_conv_b = pb[None]
    s_conv_a = _from_time_major(sa, bs)[None]
    s_conv_b = _from_time_major(sb, bs)[None]

    ffn0 = _ffn_weights(w_ffn_in[0], conv_f_w[0], w_down[0])
    wpre0 = w_out_ab[0].astype(BF16)
    zeros_f = jnp.zeros((bp, N_FF_CHUNKS, KF - 1, FF_CHUNK), F32)
    hp, pf0 = _ffn_call(hp, act_p, wpre0, row(g_ffn[0]), *ffn0, zeros_f, gfin,
                        tt=tt_p, stride=1, final_norm=False)
    hs, sf0 = _ffn_call(hs, act_s, wpre0, row(g_ffn[0]), *ffn0,
                        _ffn_state_in(_to_time_major(state_ffn[0])), gfin,
                        tt=rows_s, stride=bs, final_norm=False)

    wqkv = w_qkv[0].astype(BF16)
    qp, kp, vp = _qkv_call(hp, row(g_mix[1]), wqkv, tt=tt_p)
    att_p = _attn_prompt_call(slopes, qp, kp, vp)
    keep = min(BRANCHES[-1][0], seq)
    p_k = kp[:, seq - keep:].reshape(1, bp, keep, N_HEADS, HEAD_DIM)
    p_v = vp[:, seq - keep:].reshape(1, bp, keep, N_HEADS, HEAD_DIM)

    qs, ks, vs = _qkv_call(hs, row(g_mix[1]), wqkv, tt=rows_s)
    pad8 = lambda x: jnp.pad(_from_time_major(x, bs), ((0, 0), (0, V7X_SUBLANES - t_new), (0, 0)))
    att_s, s_k, s_v = _attn_sample_call(
        slopes, pad8(qs), pad8(ks), pad8(vs),
        cache_k[0].reshape(bs, buf_len, d), cache_v[0].reshape(bs, buf_len, d), t_new=t_new)
    att_s = _to_time_major(att_s[:, :t_new]).astype(BF16)
    s_k = s_k.reshape(1, bs, buf_len, N_HEADS, HEAD_DIM)
    s_v = s_v.reshape(1, bs, buf_len, N_HEADS, HEAD_DIM)

    ffn1 = _ffn_weights(w_ffn_in[1], conv_f_w[1], w_down[1])
    wpre1 = w_o[0].astype(BF16)
    hp, pf1 = _ffn_call(hp, att_p, wpre1, row(g_ffn[1]), *ffn1, zeros_f, gfin,
                        tt=tt_p, stride=1, final_norm=True)
    hs, sf1 = _ffn_call(hs, att_s, wpre1, row(g_ffn[1]), *ffn1,
                        _ffn_state_in(_to_time_major(state_ffn[1])), gfin,
                        tt=rows_s, stride=bs, final_norm=True)

    y_prompt = hp
    y_sample = _from_time_major(hs, bs)
    p_ffn = jnp.stack([_ffn_state_out(pf0), _ffn_state_out(pf1)])
    s_ffn = jnp.stack([_from_time_major(_ffn_state_out(sf0), bs),
                       _from_time_major(_ffn_state_out(sf1), bs)])
    return (y_prompt, y_sample, p_conv_a, s_conv_a, p_conv_b, s_conv_b,
            p_k, s_k, p_v, s_v, p_ffn, s_ffn)
```

```python
import functools

import numpy as np
import jax
import jax.numpy as jnp
from jax import lax
from jax.experimental import pallas as pl
from jax.experimental.pallas import tpu as pltpu

D_MODEL = 1024
CONV_CH = 512
KA, KB, KF = 31, 3, 3
HEAD_DIM = 64
N_HEADS = D_MODEL // HEAD_DIM
N_PAIRS = N_HEADS // 2
D_FF = 2816
FF_CHUNK = 256
N_FF_CHUNKS = D_FF // FF_CHUNK
BRANCHES = ((128, 1), (512, 4), (2048, 16))
Q_BLOCK = 128
RMS_EPS = 1e-6
LN_EPS = 1e-5
NEG_BIG = -1e30

V7X_SUBLANES = 8
V7X_LANES = 128
VMEM_LIMIT_BYTES = 56 * 1024 * 1024

BF16 = jnp.bfloat16
F32 = jnp.float32


def _round_up(x, m):
    return (x + m - 1) // m * m


def _dot(a, b):
    return jnp.dot(a, b, preferred_element_type=F32)


def _dot_nt(a, b):
    return lax.dot_general(a, b, (((1,), (1,)), ((), ())), preferred_element_type=F32)


def _rms(x, g):
    return x * lax.rsqrt(jnp.mean(x * x, axis=-1, keepdims=True) + RMS_EPS) * g


def _sigmoid(x):
    return 1.0 / (1.0 + jnp.exp(-x))


def _lanes(j):
    return slice(j * V7X_LANES, (j + 1) * V7X_LANES)


def _const_spec(shape):
    nd = len(shape)
    return pl.BlockSpec(shape, lambda *_: (0,) * nd, pipeline_mode=pl.Buffered(1))


def _params(n_grid):
    return pltpu.CompilerParams(
        dimension_semantics=("arbitrary",) * n_grid, vmem_limit_bytes=VMEM_LIMIT_BYTES)


def _slab_conv(win, w_ref, j, n_taps, base, stride, rows):
    acc = w_ref[0:1, _lanes(j)] * win[base:base + rows, :]
    for k in range(1, n_taps):
        off = base + k * stride
        acc = acc + w_ref[k:k + 1, _lanes(j)] * win[off:off + rows, :]
    return acc


def _mixer_kernel(h_ref, sta_ref, stb_ref, g_ref, win_ref, wa_ref, ba_ref, lg_ref, lb_ref, wb_ref,
                  act_ref, nsa_ref, nsb_ref, hist_a, hist_b, gate_b, *, tt, stride, nt, rc):
    n_slabs = hist_a.shape[0]
    pa = hist_a.shape[1] - tt
    pb = hist_b.shape[1] - tt
    base_a = pa - (KA - 1) * stride
    base_b = pb - (KB - 1) * stride
    t = pl.program_id(1)
    c = CONV_CH

    @pl.when(t == 0)
    def _():
        for j in range(n_slabs):
            hist_a[j, base_a:pa, :] = sta_ref[:, _lanes(j)]
            hist_b[j, base_b:pb, :] = stb_ref[:, _lanes(j)]

    xb = _rms(h_ref[...], g_ref[...]).astype(BF16)
    glu = _dot(xb, win_ref[:, 0:c]) * _sigmoid(_dot(xb, win_ref[:, c:2 * c]))
    gate_b[...] = _dot(xb, win_ref[:, 2 * c:3 * c])
    cx = _dot(xb, win_ref[:, 3 * c:4 * c]) * _dot(xb, win_ref[:, 4 * c:5 * c])
    for j in range(n_slabs):
        hist_a[j, pa:pa + tt, :] = glu[:, _lanes(j)]
        hist_b[j, pb:pb + tt, :] = cx[:, _lanes(j)]

    def chunk(i, carry):
        r0 = pl.multiple_of(i * rc, rc)
        af = jnp.concatenate(
            [_slab_conv(hist_a.at[j, pl.ds(r0, rc + pa)], wa_ref, j, KA, base_a, stride, rc)
             for j in range(n_slabs)], axis=1) + ba_ref[...]
        mu = jnp.mean(af, axis=-1, keepdims=True)
        dev = af - mu
        var = jnp.mean(dev * dev, axis=-1, keepdims=True)
        an = dev * lax.rsqrt(var + LN_EPS) * lg_ref[...] + lb_ref[...]
        act_ref[pl.ds(r0, rc), 0:c] = (an * _sigmoid(an)).astype(BF16)
        cb = jnp.concatenate(
            [_slab_conv(hist_b.at[j, pl.ds(r0, rc + pb)], wb_ref, j, KB, base_b, stride, rc)
             for j in range(n_slabs)], axis=1)
        act_ref[pl.ds(r0, rc), c:2 * c] = (gate_b[pl.ds(r0, rc), :] * cb).astype(BF16)
        return carry

    lax.fori_loop(0, tt // rc, chunk, 0)

    @pl.when(t == nt - 1)
    def _():
        for j in range(n_slabs):
            nsa_ref[:, _lanes(j)] = hist_a[j, pa + tt - (KA - 1) * stride:pa + tt, :]
            nsb_ref[:, _lanes(j)] = hist_b[j, pb + tt - (KB - 1) * stride:pb + tt, :]

    if nt > 1:
        for j in range(n_slabs):
            hist_a[j, 0:pa, :] = hist_a[j, tt:tt + pa, :]
            hist_b[j, 0:pb, :] = hist_b[j, tt:tt + pb, :]


def _mixer_call(h, sta, stb, g, win, wa, ba, lg, lb, wb, *, tt, stride):
    nb, rows, d = h.shape
    nt = rows // tt
    assert rows % tt == 0
    rc = 64
    pa = _round_up((KA - 1) * stride, V7X_SUBLANES)
    pb = _round_up((KB - 1) * stride, V7X_SUBLANES)
    assert nt == 1 or tt >= pa
    c = CONV_CH
    n_slabs = c // V7X_LANES
    tile = lambda w: pl.BlockSpec((None, tt, w), lambda b, t: (b, t, 0))
    per_b = lambda r, w: pl.BlockSpec((None, r, w), lambda b, t: (b, 0, 0))
    kern = functools.partial(_mixer_kernel, tt=tt, stride=stride, nt=nt, rc=rc)
    return pl.pallas_call(
        kern,
        grid=(nb, nt),
        in_specs=[tile(d), per_b((KA - 1) * stride, c), per_b((KB - 1) * stride, c),
                  _const_spec(g.shape), _const_spec(win.shape), _const_spec(wa.shape),
                  _const_spec(ba.shape), _const_spec(lg.shape), _const_spec(lb.shape),
                  _const_spec(wb.shape)],
        out_specs=[tile(2 * c), per_b((KA - 1) * stride, c), per_b((KB - 1) * stride, c)],
        out_shape=[jax.ShapeDtypeStruct((nb, rows, 2 * c), BF16),
                   jax.ShapeDtypeStruct((nb, (KA - 1) * stride, c), F32),
                   jax.ShapeDtypeStruct((nb, (KB - 1) * stride, c), F32)],
        scratch_shapes=[pltpu.VMEM((n_slabs, pa + tt, V7X_LANES), F32),
                        pltpu.VMEM((n_slabs, pb + tt, V7X_LANES), F32),
                        pltpu.VMEM((tt, c), F32)],
        compiler_params=_params(2),
        name="mixer",
    )(h, sta, stb, g, win, wa, ba, lg, lb, wb)


def _ffn_kernel(h_ref, act_ref, wpre_ref, g_ref, wg_ref, wu_ref, wc_ref, wd_ref, st_ref, gfin_ref,
                o_ref, nst_ref, carry, hist0, hist1, up0, up1, xb_s, acc,
                *, tt, stride, nt, final_norm):
    n_slabs = hist0.shape[0]
    pf = hist0.shape[1] - tt
    base = pf - (KF - 1) * stride
    t = pl.program_id(1)
    stage = ((hist0, up0), (hist1, up1))

    @pl.when(t == 0)
    def _():
        for j in range(n_slabs):
            carry[:, j, base:pf, :] = st_ref[:, :, _lanes(j)]

    h1 = h_ref[...] + _dot(act_ref[...], wpre_ref[...])
    xb_s[...] = _rms(h1, g_ref[...]).astype(BF16)
    acc[...] = h1

    def project(c, slot):
        hist, up = stage[slot]
        xb = xb_s[...]
        gate = _dot(xb, wg_ref[c])
        for j in range(n_slabs):
            hist[j, 0:pf, :] = carry[c, j]
            hist[j, pf:pf + tt, :] = gate[:, _lanes(j)]
            carry[c, j] = hist[j, tt:tt + pf, :]
        up[...] = _dot(xb, wu_ref[c])

    def activate(c, slot):
        hist, up = stage[slot]
        wc = wc_ref.at[c]
        gc = jnp.concatenate(
            [_slab_conv(hist.at[j], wc, j, KF, base, stride, tt) for j in range(n_slabs)], axis=1)
        return (gc * _sigmoid(gc) * up[...]).astype(BF16)

    def step(c, slot):
        a = activate(c, slot)
        project(c + 1, 1 - slot)
        acc[...] += _dot(a, wd_ref[c])

    project(0, 0)

    def pair(i, _):
        step(2 * i, 0)
        step(2 * i + 1, 1)
        return 0

    assert N_FF_CHUNKS % 2 == 1
    lax.fori_loop(0, N_FF_CHUNKS // 2, pair, 0)
    last = N_FF_CHUNKS - 1
    acc[...] += _dot(activate(last, 0), wd_ref[last])

    out = acc[...]
    if final_norm:
        out = _rms(out, gfin_ref[...])
    o_ref[...] = out

    @pl.when(t == nt - 1)
    def _():
        for j in range(n_slabs):
            nst_ref[:, :, _lanes(j)] = carry[:, j, base:pf, :]


def _ffn_call(h, act, wpre, g, wg, wu, wc, wd, st, gfin, *, tt, stride, final_norm):
    nb, rows, d = h.shape
    nt = rows // tt
    assert rows % tt == 0
    pf = _round_up((KF - 1) * stride, V7X_SUBLANES)
    assert tt >= pf
    srows = (KF - 1) * stride
    n_slabs = FF_CHUNK // V7X_LANES
    tile = pl.BlockSpec((None, tt, d), lambda b, t: (b, t, 0))
    st_spec = pl.BlockSpec((None, N_FF_CHUNKS, srows, FF_CHUNK), lambda b, t: (b, 0, 0, 0))
    kern = functools.partial(_ffn_kernel, tt=tt, stride=stride, nt=nt, final_norm=final_norm)
    hist = pltpu.VMEM((n_slabs, pf + tt, V7X_LANES), F32)
    up = pltpu.VMEM((tt, FF_CHUNK), F32)
    return pl.pallas_call(
        kern,
        grid=(nb, nt),
        in_specs=[tile, tile, _const_spec(wpre.shape), _const_spec(g.shape), _const_spec(wg.shape),
                  _const_spec(wu.shape), _const_spec(wc.shape), _const_spec(wd.shape), st_spec,
                  _const_spec(gfin.shape)],
        out_specs=[tile, st_spec],
        out_shape=[jax.ShapeDtypeStruct((nb, rows, d), F32),
                   jax.ShapeDtypeStruct((nb, N_FF_CHUNKS, srows, FF_CHUNK), F32)],
        scratch_shapes=[pltpu.VMEM((N_FF_CHUNKS, n_slabs, pf, V7X_LANES), F32),
                        hist, hist, up, up,
                        pltpu.VMEM((tt, d), BF16),
                        pltpu.VMEM((tt, d), F32)],
        compiler_params=_params(2),
        name="ffn",
    )(h, act, wpre, g, wg, wu, wc, wd, st, gfin)


def _qkv_kernel(h_ref, g_ref, w_ref, q_ref, k_ref, v_ref):
    xb = _rms(h_ref[...], g_ref[...]).astype(BF16)
    d = D_MODEL
    q_ref[...] = _dot(xb, w_ref[:, 0:d])
    k_ref[...] = _dot(xb, w_ref[:, d:2 * d])
    v_ref[...] = _dot(xb, w_ref[:, 2 * d:3 * d])


def _qkv_call(h, g, w, *, tt):
    nb, rows, d = h.shape
    assert rows % tt == 0
    tile = pl.BlockSpec((None, tt, d), lambda b, t: (b, t, 0))
    out = jax.ShapeDtypeStruct((nb, rows, d), F32)
    return pl.pallas_call(
        _qkv_kernel,
        grid=(nb, rows // tt),
        in_specs=[tile, _const_spec(g.shape), _const_spec(w.shape)],
        out_specs=[tile, tile, tile],
        out_shape=[out, out, out],
        compiler_params=_params(2),
        name="qkv",
    )(h, g, w)


def _attn_prompt_kernel(slopes_ref, q_ref, k_ref, v_ref, o_ref, acc_s, m_even, m_odd, l_even, l_odd,
                        bias_two, bias_one, *, seq):
    pair = pl.program_id(1)
    qb = Q_BLOCK
    row2 = lax.broadcasted_iota(jnp.int32, (2 * qb, 1), 0)
    slope = jnp.where(row2 < qb, slopes_ref[2 * pair], slopes_ref[2 * pair + 1])
    lane = lax.broadcasted_iota(jnp.int32, (1, V7X_LANES), 1)
    low = lane < HEAD_DIM
    ones_cols = jnp.ones((2 * qb, V7X_LANES), BF16)

    qi_idx = lax.broadcasted_iota(jnp.int32, (2 * qb, 2 * qb), 0) % qb
    kj_idx = lax.broadcasted_iota(jnp.int32, (2 * qb, 2 * qb), 1)
    rel_two = qi_idx + qb - kj_idx
    rel_one = (qi_idx - kj_idx)[:, :qb]
    for g, (_, dil) in enumerate(BRANCHES):
        bias_one[g] = jnp.where(rel_one >= 0, -slope * (dil * rel_one).astype(F32), NEG_BIG)
        if seq // dil // qb > 1:
            bias_two[g] = jnp.where((rel_two >= 0) & (rel_two <= qb),
                                    -slope * (dil * rel_two).astype(F32), NEG_BIG)

    def group(g, starts, first, fresh, final):
        dil = BRANCHES[g][1]
        ds = lambda s, n: pl.ds(s, n, stride=dil) if dil > 1 else pl.ds(s, n)
        rows = [ds(s, qb) for s in starts]
        old = [] if fresh else [
            (jnp.concatenate([m_even[r, :], m_odd[r, :]], axis=0),
             jnp.concatenate([l_even[r, :], l_odd[r, :]], axis=0), acc_s[r, :]) for r in rows]
        new = []
        for i, start in enumerate(starts):
            q = q_ref[rows[i], :] * (1.0 / np.sqrt(HEAD_DIM))
            q2 = jnp.concatenate([jnp.where(low, q, 0.0), jnp.where(low, 0.0, q)], axis=0).astype(BF16)
            if first:
                krows, bias = rows[i], bias_one[g]
            else:
                krows, bias = ds(start - dil * qb, 2 * qb), bias_two[g]
            kb = k_ref[krows, :].astype(BF16)
            n_keys = kb.shape[0]
            vb = jnp.concatenate([v_ref[krows, :].astype(BF16), ones_cols[:n_keys]], axis=1)
            s = _dot_nt(q2, kb) + bias
            m_new = jnp.broadcast_to(jnp.max(s, axis=-1, keepdims=True), (2 * qb, V7X_LANES))
            if not fresh:
                m_new = jnp.maximum(old[i][0], m_new)
            p = jnp.exp(s - jnp.concatenate([m_new] * (n_keys // V7X_LANES), axis=1)).astype(BF16)
            pv = _dot(p, vb)
            acc_new = jnp.where(low, pv[:qb, :V7X_LANES], pv[qb:, :V7X_LANES])
            l_new = pv[:, V7X_LANES:]
            if not fresh:
                alpha = jnp.exp(old[i][0] - m_new)
                acc_new = jnp.where(low, alpha[:qb], alpha[qb:]) * old[i][2] + acc_new
                l_new = alpha * old[i][1] + l_new
            new.append((m_new, l_new, acc_new))
        for r, (m_new, l_new, acc_new) in zip(rows, new):
            if final:
                o_ref[r, :] = (acc_new / jnp.where(low, l_new[:qb], l_new[qb:])).astype(o_ref.dtype)
            else:
                m_even[r, :] = m_new[:qb]
                m_odd[r, :] = m_new[qb:]
                l_even[r, :] = l_new[:qb]
                l_odd[r, :] = l_new[qb:]
                acc_s[r, :] = acc_new

    order = list(range(len(BRANCHES)))[::-1]
    assert BRANCHES[order[-1]][1] == 1
    for g in order:
        window, dil = BRANCHES[g]
        assert window // dil == qb
        fresh, final = g == order[0], g == order[-1]
        n_q = seq // dil // qb
        per_body = 4 if dil >= 4 else 1
        assert dil % per_body == 0
        if dil <= per_body:
            group(g, list(range(dil)), True, fresh, final)
        else:
            def first_body(i, _, g=g, per_body=per_body, fresh=fresh, final=final):
                group(g, [i * per_body + u for u in range(per_body)], True, fresh, final)
                return 0
            lax.fori_loop(0, dil // per_body, first_body, 0)
        if n_q > 1:
            if dil == 1:
                per_iter = 3
                assert (n_q - 1) % per_iter == 0

                def body(i, _, g=g, per_iter=per_iter, fresh=fresh, final=final):
                    starts = [pl.multiple_of((1 + i * per_iter + u) * qb, qb) for u in range(per_iter)]
                    group(g, starts, False, fresh, final)
                    return 0
                lax.fori_loop(0, (n_q - 1) // per_iter, body, 0)
            else:
                def body(qi, _, g=g, dil=dil, fresh=fresh, final=final):
                    group(g, [r + dil * qb * qi for r in range(dil)], False, fresh, final)
                    return 0
                lax.fori_loop(1, n_q, body, 0)


def _attn_prompt_call(slopes, q, k, v):
    nb, seq, d = q.shape
    assert BRANCHES[1][1] == 4, "dilation-4 bodies trace one block per residue"
    slab = pl.BlockSpec((None, seq, V7X_LANES), lambda b, p: (b, 0, p))
    scratch = pltpu.VMEM((seq, V7X_LANES), F32)
    n_br = len(BRANCHES)
    return pl.pallas_call(
        functools.partial(_attn_prompt_kernel, seq=seq),
        grid=(nb, N_PAIRS),
        in_specs=[pl.BlockSpec(memory_space=pltpu.SMEM), slab, slab, slab],
        out_specs=slab,
        out_shape=jax.ShapeDtypeStruct((nb, seq, d), BF16),
        scratch_shapes=[scratch, scratch, scratch, scratch, scratch,
                        pltpu.VMEM((n_br, 2 * Q_BLOCK, 2 * Q_BLOCK), F32),
                        pltpu.VMEM((n_br, 2 * Q_BLOCK, Q_BLOCK), F32)],
        compiler_params=_params(2),
        name="attn_prompt",
    )(slopes, q, k, v)


def _attn_sample_kernel(slopes_ref, q_ref, kn_ref, vn_ref, ck_ref, cv_ref, o_ref, ok_ref, ov_ref,
                        *, buf_len, t_new, heads):
    hb = pl.program_id(1)
    rows = q_ref.shape[1]
    tail = V7X_LANES - t_new
    tq = lax.broadcasted_iota(jnp.int32, (rows, 1), 0)
    rel_c = buf_len + tq - lax.broadcasted_iota(jnp.int32, (rows, buf_len), 1)
    col_n = lax.broadcasted_iota(jnp.int32, (rows, V7X_LANES), 1) - tail
    rel_n = tq - col_n

    def count(rel, extra=None):
        cnt = jnp.zeros(rel.shape, F32)
        for window, dil in BRANCHES:
            ok = (rel >= 0) & (rel <= window) & ((rel & (dil - 1)) == 0)
            if extra is not None:
                ok = ok & extra
            cnt = cnt + jnp.where(ok, 1.0, 0.0)
        return cnt

    cnt_c = count(rel_c)
    cnt_n = count(rel_n, col_n >= 0)
    off_c = jnp.where(cnt_c > 0, 0.0, NEG_BIG)
    off_n = jnp.where(cnt_n > 0, 0.0, NEG_BIG)
    relf_c = rel_c.astype(F32)
    relf_n = rel_n.astype(F32)
    lane = lax.broadcasted_iota(jnp.int32, (1, V7X_LANES), 1)
    body_w = buf_len - V7X_LANES

    def shifted(old, new_tile):
        rolled = pltpu.roll(old, buf_len - t_new, axis=1)
        last = jnp.where(lane >= tail, new_tile, rolled[:, body_w:])
        return jnp.concatenate([rolled[:, :body_w], last], axis=1)

    for hh in range(heads):
        slope = slopes_ref[hb * heads + hh]
        qh = (q_ref[hh] * (1.0 / np.sqrt(HEAD_DIM))).astype(BF16)
        kt, vt = ck_ref[hh], cv_ref[hh]
        knt, vnt = kn_ref[hh], vn_ref[hh]
        s_c = _dot(qh, kt.astype(BF16)) - slope * relf_c + off_c
        s_n = _dot(qh, knt.astype(BF16)) - slope * relf_n + off_n
        m = jnp.maximum(jnp.max(s_c, axis=-1, keepdims=True), jnp.max(s_n, axis=-1, keepdims=True))
        p_c = cnt_c * jnp.exp(s_c - m)
        p_n = cnt_n * jnp.exp(s_n - m)
        l = jnp.sum(p_c, axis=-1, keepdims=True) + jnp.sum(p_n, axis=-1, keepdims=True)
        pv = _dot_nt(p_c.astype(BF16), vt.astype(BF16)) + _dot_nt(p_n.astype(BF16), vnt.astype(BF16))
        o_ref[hh] = pv / l
        ok_ref[hh] = shifted(kt, knt)
        ov_ref[hh] = shifted(vt, vnt)


def _attn_sample_call(slopes, q, kn, vn, ck, cv, *, t_new):
    nb, nh, rows, hd = q.shape
    buf_len = ck.shape[3]
    assert buf_len >= BRANCHES[-1][0], "every branch position must fall inside the window buffer"
    assert t_new <= rows and buf_len % V7X_LANES == 0
    heads = 4
    spec = lambda r, w: pl.BlockSpec((None, heads, r, w), lambda b, h: (b, h, 0, 0))
    big = jax.ShapeDtypeStruct((nb, nh, hd, buf_len), F32)
    return pl.pallas_call(
        functools.partial(_attn_sample_kernel, buf_len=buf_len, t_new=t_new, heads=heads),
        grid=(nb, nh // heads),
        in_specs=[pl.BlockSpec(memory_space=pltpu.SMEM), spec(rows, hd), spec(hd, V7X_LANES),
                  spec(hd, V7X_LANES), spec(hd, buf_len), spec(hd, buf_len)],
        out_specs=[spec(rows, hd), spec(hd, buf_len), spec(hd, buf_len)],
        out_shape=[jax.ShapeDtypeStruct((nb, nh, rows, hd), F32), big, big],
        compiler_params=_params(2),
        name="attn_sample",
    )(slopes, q, kn, vn, ck, cv)


def _pad_rows(w, rows):
    return jnp.pad(w, ((0, rows - w.shape[0]), (0, 0)))


def _ffn_weights(w_in, conv_w, w_down):
    d = w_in.shape[0]
    wg = w_in[:, :D_FF].astype(BF16).reshape(d, N_FF_CHUNKS, FF_CHUNK).transpose(1, 0, 2)
    wu = w_in[:, D_FF:].astype(BF16).reshape(d, N_FF_CHUNKS, FF_CHUNK).transpose(1, 0, 2)
    wc = _pad_rows(conv_w, V7X_SUBLANES).reshape(V7X_SUBLANES, N_FF_CHUNKS, FF_CHUNK).transpose(1, 0, 2)
    wd = w_down.astype(BF16).reshape(N_FF_CHUNKS, FF_CHUNK, w_down.shape[1])
    return wg, wu, wc, wd


def _to_time_major(x):
    b, t, c = x.shape
    return x.transpose(1, 0, 2).reshape(1, t * b, c)


def _from_time_major(x, b):
    _, rows, c = x.shape
    return x.reshape(rows // b, b, c).transpose(1, 0, 2)


def _ffn_state_in(st):
    nb, rows, _ = st.shape
    return st.reshape(nb, rows, N_FF_CHUNKS, FF_CHUNK).transpose(0, 2, 1, 3)


def _ffn_state_out(st):
    nb, _, rows, _ = st.shape
    return st.transpose(0, 2, 1, 3).reshape(nb, rows, D_FF)


def _heads_rows(x, bs, rows):
    t = x.shape[1] // bs
    x = x.reshape(t, bs, N_HEADS, HEAD_DIM).transpose(1, 2, 0, 3)
    return jnp.pad(x, ((0, 0), (0, 0), (0, rows - t), (0, 0)))


def _heads_tail_tile(x, bs):
    t = x.shape[1] // bs
    x = x.reshape(t, bs, N_HEADS, HEAD_DIM).transpose(1, 2, 3, 0)
    return jnp.pad(x, ((0, 0), (0, 0), (0, 0), (V7X_LANES - t, 0)))


def kernel(x_prompt, x_sample, state_conv_a, state_conv_b, cache_k, cache_v, state_ffn, g_mix, g_ffn, g_final, w_in_ab, conv_a_w, conv_a_b, ln_a_g, ln_a_b, conv_b_w, w_out_ab, w_qkv, w_o, w_ffn_in, conv_f_w, w_down):
    bp, seq, d = x_prompt.shape
    bs, t_new, _ = x_sample.shape
    tt_p = 512
    rows_s = bs * t_new
    slopes = jnp.asarray(2.0 ** (-8.0 * np.arange(1, N_HEADS + 1) / N_HEADS), dtype=F32)
    row = lambda v: v.reshape(1, -1)

    hp = x_prompt
    hs = _to_time_major(x_sample)
    gfin = row(g_final)

    win = w_in_ab[0].astype(BF16)
    wa = _pad_rows(conv_a_w[0], _round_up(KA, V7X_SUBLANES))
    wb = _pad_rows(conv_b_w[0], V7X_SUBLANES)
    mix_args = (row(g_mix[0]), win, wa, row(conv_a_b[0]), row(ln_a_g[0]), row(ln_a_b[0]), wb)
    zeros_a = jnp.zeros((bp, KA - 1, CONV_CH), F32)
    zeros_b = jnp.zeros((bp, KB - 1, CONV_CH), F32)
    act_p, pa, pb = _mixer_call(hp, zeros_a, zeros_b, *mix_args, tt=tt_p, stride=1)
    act_s, sa, sb = _mixer_call(hs, _to_time_major(state_conv_a[0]), _to_time_major(state_conv_b[0]),
                                *mix_args, tt=rows_s, stride=bs)
    p_conv_a = pa[None]
    p_conv_b = pb[None]
    s_conv_a = _from_time_major(sa, bs)[None]
    s_conv_b = _from_time_major(sb, bs)[None]

    ffn0 = _ffn_weights(w_ffn_in[0], conv_f_w[0], w_down[0])
    wpre0 = w_out_ab[0].astype(BF16)
    zeros_f = jnp.zeros((bp, N_FF_CHUNKS, KF - 1, FF_CHUNK), F32)
    hp, pf0 = _ffn_call(hp, act_p, wpre0, row(g_ffn[0]), *ffn0, zeros_f, gfin,
                        tt=tt_p, stride=1, final_norm=False)
    hs, sf0 = _ffn_call(hs, act_s, wpre0, row(g_ffn[0]), *ffn0,
                        _ffn_state_in(_to_time_major(state_ffn[0])), gfin,
                        tt=rows_s, stride=bs, final_norm=False)

    wqkv = w_qkv[0].astype(BF16)
    qp, kp, vp = _qkv_call(hp, row(g_mix[1]), wqkv, tt=tt_p)
    att_p = _attn_prompt_call(slopes, qp, kp, vp)
    keep = min(BRANCHES[-1][0], seq)
    p_k = kp[:, seq - keep:].reshape(1, bp, keep, N_HEADS, HEAD_DIM)
    p_v = vp[:, seq - keep:].reshape(1, bp, keep, N_HEADS, HEAD_DIM)

    qs, ks, vs = _qkv_call(hs, row(g_mix[1]), wqkv, tt=rows_s)
    att_s, s_k, s_v = _attn_sample_call(
        slopes, _heads_rows(qs, bs, V7X_SUBLANES), _heads_tail_tile(ks, bs), _heads_tail_tile(vs, bs),
        cache_k[0].transpose(0, 2, 3, 1), cache_v[0].transpose(0, 2, 3, 1), t_new=t_new)
    att_s = att_s[:, :, :t_new].transpose(2, 0, 1, 3).reshape(1, rows_s, d).astype(BF16)
    s_k = s_k.transpose(0, 3, 1, 2)[None]
    s_v = s_v.transpose(0, 3, 1, 2)[None]

    ffn1 = _ffn_weights(w_ffn_in[1], conv_f_w[1], w_down[1])
    wpre1 = w_o[0].astype(BF16)
    hp, pf1 = _ffn_call(hp, att_p, wpre1, row(g_ffn[1]), *ffn1, zeros_f, gfin,
                        tt=tt_p, stride=1, final_norm=True)
    hs, sf1 = _ffn_call(hs, att_s, wpre1, row(g_ffn[1]), *ffn1,
                        _ffn_state_in(_to_time_major(state_ffn[1])), gfin,
                        tt=rows_s, stride=bs, final_norm=True)

    y_prompt = hp
    y_sample = _from_time_major(hs, bs)
    p_ffn = jnp.stack([_ffn_state_out(pf0), _ffn_state_out(pf1)])
    s_ffn = jnp.stack([_from_time_major(_ffn_state_out(sf0), bs),
                       _from_time_major(_ffn_state_out(sf1), bs)])
    return (y_prompt, y_sample, p_conv_a, s_conv_a, p_conv_b, s_conv_b,
            p_k, s_k, p_v, s_v, p_ffn, s_ffn)
```
